```python
import jax, jax.numpy as jnp
from jax import lax
import numpy as np

D_MODEL = 2048
BATCH = 4
SEQ = 4096
DEPTH = 4

D_MIX = D_MODEL
D_RG = D_MIX // 2
RG_BLOCKS = 8
RG_BLOCK_W = D_RG // RG_BLOCKS
CONV_W = 4
RG_C = 8.0
D_GLA_V = D_MIX - D_RG
GLA_HEADS = 4
D_GLA_K = D_GLA_V // 2
GLA_HK = D_GLA_K // GLA_HEADS
GLA_HV = D_GLA_V // GLA_HEADS
GLA_RANK = 16
GLA_TAU = 16.0
GLA_CHUNK = 64
D_IN = 2 * D_RG + 2 * D_GLA_K + 2 * D_GLA_V + GLA_RANK
D_FF = 4 * D_MODEL
N_MOD = 6
ADA_STD = 0.2
ALPHA = (2 * DEPTH) ** 0.25
BETA = (8 * DEPTH) ** -0.25
LN_EPS = 1e-5

kernel_name = "hybrid_rglru_gla_deepnorm_adaln"


def _layer_norm(x, g, b):
    xf = x.astype(jnp.float32)
    mu = jnp.mean(xf, axis=-1, keepdims=True)
    var = jnp.mean(jnp.square(xf - mu), axis=-1, keepdims=True)
    return ((xf - mu) * lax.rsqrt(var + LN_EPS) * g + b).astype(x.dtype)


def _causal_conv(x, w, b):
    S = x.shape[1]
    xp = jnp.pad(x, ((0, 0), (CONV_W - 1, 0), (0, 0)))
    y = b + xp[:, 0:S] * w[0]
    for j in range(1, CONV_W):
        y = y + xp[:, j:j + S] * w[j]
    return y


def _rg_lru(x, w_a, b_a, w_x, b_x, lam):
    B, S, _ = x.shape
    xb = x.reshape(B, S, RG_BLOCKS, RG_BLOCK_W)
    r = jax.nn.sigmoid(jnp.einsum('bshi,hij->bshj', xb, w_a).reshape(B, S, D_RG) + b_a)
    i = jax.nn.sigmoid(jnp.einsum('bshi,hij->bshj', xb, w_x).reshape(B, S, D_RG) + b_x)
    f32 = jnp.float32
    log_a = -RG_C * r.astype(f32) * jax.nn.softplus(-lam.astype(f32))
    a = jnp.exp(log_a)
    u = jnp.sqrt(-jnp.expm1(2.0 * log_a)) * i.astype(f32) * x.astype(f32)

    def combine(left, right):
        a1, b1 = left
        a2, b2 = right
        return a1 * a2, a2 * b1 + b2

    _, h = lax.associative_scan(combine, (a, u), axis=1)
    return h.astype(x.dtype)


def _gla(q, k, v, glr, w_g2, b_g2, norm_g, r):
    B, S, _ = q.shape
    C = GLA_CHUNK
    N = S // C
    f32 = jnp.float32
    z = (glr @ w_g2 + b_g2).astype(f32)
    g = jax.nn.log_sigmoid(z) / GLA_TAU
    shp_k = (B, N, C, GLA_HEADS, GLA_HK)
    qf = q.astype(f32).reshape(shp_k) * (GLA_HK ** -0.5)
    kf = k.astype(f32).reshape(shp_k)
    vf = v.astype(f32).reshape(B, N, C, GLA_HEADS, GLA_HV)
    bcum = jnp.cumsum(g.reshape(shp_k), axis=2)
    b_last = bcum[:, :, -1:]
    q_dec = qf * jnp.exp(bcum)
    k_inv = kf * jnp.exp(-bcum)
    k_tail = kf * jnp.exp(b_last - bcum)
    causal = jnp.tril(jnp.ones((C, C), dtype=bool))
    att = jnp.einsum('bnihd,bnjhd->bnhij', q_dec, k_inv)
    att = jnp.where(causal, att, 0.0)
    o_intra = jnp.einsum('bnhij,bnjhv->bnihv', att, vf)
    chunk_update = jnp.einsum('bnjhd,bnjhv->nbhdv', k_tail, vf)
    chunk_decay = jnp.exp(jnp.moveaxis(b_last[:, :, 0], 1, 0))

    def step(state, inp):
        dec, upd = inp
        return dec[..., None] * state + upd, state

    s0 = jnp.zeros((B, GLA_HEADS, GLA_HK, GLA_HV), f32)
    _, s_prev = lax.scan(step, s0, (chunk_decay, chunk_update))
    o_inter = jnp.einsum('bnihd,nbhdv->bnihv', q_dec, s_prev)
    o = (o_intra + o_inter).reshape(B, S, GLA_HEADS, GLA_HV)
    o = o * lax.rsqrt(jnp.mean(o * o, axis=-1, keepdims=True) + LN_EPS)
    o = o.reshape(B, S, D_GLA_V) * norm_g * jax.nn.silu(r.astype(f32))
    return o.astype(q.dtype)


def setup_inputs(seed: int = 0) -> dict:
    key = jax.random.key(seed)
    ks = jax.random.split(key, 24)
    L, D = DEPTH, D_MODEL
    nrm = jax.random.normal
    a_c = jax.random.uniform(ks[11], (L, D_RG), minval=0.9, maxval=0.999)
    s = a_c ** (1.0 / RG_C)
    rg_lambda = jnp.log(s) - jnp.log1p(-s)
    return {
        'x': nrm(ks[0], (BATCH, SEQ, D), jnp.float32),
        'c': nrm(ks[1], (BATCH, D), jnp.float32),
        'w_ada': nrm(ks[2], (L, D, N_MOD * D)) * (D ** -0.5) * ADA_STD,
        'b_ada': nrm(ks[3], (L, N_MOD * D)) * 0.01,
        'w_in': nrm(ks[4], (L, D, D_IN)) * (D ** -0.5),
        'conv_w': nrm(ks[5], (L, CONV_W, D_RG)) * (CONV_W ** -0.5),
        'conv_b': nrm(ks[6], (L, D_RG)) * 0.01,
        'w_rg_a': nrm(ks[7], (L, RG_BLOCKS, RG_BLOCK_W, RG_BLOCK_W)) * (RG_BLOCK_W ** -0.5),
        'b_rg_a': nrm(ks[8], (L, D_RG)) * 0.01,
        'w_rg_x': nrm(ks[9], (L, RG_BLOCKS, RG_BLOCK_W, RG_BLOCK_W)) * (RG_BLOCK_W ** -0.5),
        'b_rg_x': nrm(ks[10], (L, D_RG)) * 0.01,
        'rg_lambda': rg_lambda,
        'w_gla_g2': nrm(ks[12], (L, GLA_RANK, D_GLA_K)) * (GLA_RANK ** -0.5),
        'b_gla_g2': nrm(ks[13], (L, D_GLA_K)) * 0.01,
        'gla_norm_g': 1.0 + 0.02 * nrm(ks[14], (L, D_GLA_V)),
        'w_out': nrm(ks[15], (L, D_MIX, D)) * (D_MIX ** -0.5) * BETA,
        'ln1_g': 1.0 + 0.02 * nrm(ks[16], (L, D)),
        'ln1_b': nrm(ks[17], (L, D)) * 0.01,
        'w_up': nrm(ks[18], (L, D, D_FF)) * (D ** -0.5),
        'b_up': nrm(ks[19], (L, D_FF)) * 0.01,
        'w_down': nrm(ks[20], (L, D_FF, D)) * (D_FF ** -0.5) * BETA,
        'b_down': nrm(ks[21], (L, D)) * 0.01,
        'ln2_g': 1.0 + 0.02 * nrm(ks[22], (L, D)),
        'ln2_b': nrm(ks[23], (L, D)) * 0.01,
    }


def reference(x, c, w_ada, b_ada, w_in, conv_w, conv_b, w_rg_a, b_rg_a, w_rg_x, b_rg_x, rg_lambda,
              w_gla_g2, b_gla_g2, gla_norm_g, w_out, ln1_g, ln1_b, w_up, b_up, w_down, b_down,
              ln2_g, ln2_b):
    splits = [D_RG, 2 * D_RG, 2 * D_RG + D_GLA_K, 2 * D_RG + 2 * D_GLA_K,
              2 * D_RG + 2 * D_GLA_K + D_GLA_V, 2 * D_RG + 2 * D_GLA_K + 2 * D_GLA_V]
    c_act = jax.nn.silu(c)
    for l in range(DEPTH):
        mod = c_act @ w_ada[l] + b_ada[l]
        sh1, sc1, gt1, sh2, sc2, gt2 = [m[:, None, :] for m in jnp.split(mod, N_MOD, axis=-1)]

        u = x * (1.0 + sc1) + sh1
        p = u @ w_in[l]
        rg_x, rg_gate, q, k, v, r, glr = jnp.split(p, splits, axis=-1)
        rg_h = _rg_lru(_causal_conv(rg_x, conv_w[l], conv_b[l]),
                       w_rg_a[l], b_rg_a[l], w_rg_x[l], b_rg_x[l], rg_lambda[l])
        rg_out = rg_h * jax.nn.gelu(rg_gate)
        gla_out = _gla(q, k, v, glr, w_gla_g2[l], b_gla_g2[l], gla_norm_g[l], r)
        mix = jnp.concatenate([rg_out, gla_out], axis=-1) @ w_out[l]
        x = _layer_norm(ALPHA * x + (1.0 + gt1) * mix, ln1_g[l], ln1_b[l])

        u = x * (1.0 + sc2) + sh2
        hdn = jnp.square(jax.nn.relu(u @ w_up[l] + b_up[l]))
        y = hdn @ w_down[l] + b_down[l]
        x = _layer_norm(ALPHA * x + (1.0 + gt2) * y, ln2_g[l], ln2_b[l])
    return x
```

```python
import functools

import jax
import jax.numpy as jnp
from jax import lax
from jax.experimental import pallas as pl
from jax.experimental.pallas import tpu as pltpu

F32 = jnp.float32
BF16 = jnp.bfloat16

LANES = 128
SUBLANES = 8
VMEM_LIMIT_BYTES = 56 * 1024 * 1024

RG_BLOCKS = 8
CONV_W = 4
RG_C = 8.0
GLA_HEADS = 4
GLA_RANK = 16
GLA_TAU = 16.0
GLA_CHUNK = 64
N_MOD = 6
LN_EPS = 1e-5


def _params(semantics):
    return pltpu.CompilerParams(dimension_semantics=semantics,
                                vmem_limit_bytes=VMEM_LIMIT_BYTES)


def _layer_norm_rows(z, g, b):
    mu = jnp.mean(z, axis=-1, keepdims=True)
    zc = z - mu
    var = jnp.mean(zc * zc, axis=-1, keepdims=True)
    return zc * lax.rsqrt(var + LN_EPS) * g + b


def _log_sigmoid(z):
    return jnp.minimum(z, 0.0) - jnp.log1p(jnp.exp(-jnp.abs(z)))


def _softplus(z):
    return jnp.maximum(z, 0.0) + jnp.log1p(jnp.exp(-jnp.abs(z)))


def _ada_kernel(c_ref, w_ref, b_ref, o_ref):
    c = c_ref[...]
    ca = (c * jax.nn.sigmoid(c)).astype(BF16)
    o_ref[0] = jnp.dot(ca, w_ref[0].astype(BF16), preferred_element_type=F32) + b_ref[0]


def _ada_call(c_pad, w_ada, b_ada3, *, tn):
    depth, d, n = w_ada.shape
    rows = c_pad.shape[0]
    return pl.pallas_call(
        _ada_kernel,
        grid=(depth, n // tn),
        in_specs=[
            pl.BlockSpec((rows, d), lambda l, j: (0, 0)),
            pl.BlockSpec((1, d, tn), lambda l, j: (l, 0, j)),
            pl.BlockSpec((1, 1, tn), lambda l, j: (l, 0, j)),
        ],
        out_specs=pl.BlockSpec((1, rows, tn), lambda l, j: (l, 0, j)),
        out_shape=jax.ShapeDtypeStruct((depth, rows, n), F32),
        compiler_params=_params(("arbitrary", "arbitrary")),
        name="ada_mod",
    )(c_pad, w_ada, b_ada3)


def _inproj_kernel(x_ref, sc_ref, sh_ref, w_ref, wglr_ref, wg2_ref, bg2_ref,
                   p_ref, g_ref, u_scr):
    @pl.when(pl.program_id(2) == 0)
    def _():
        u = x_ref[0] * (1.0 + sc_ref[0]) + sh_ref[0]
        ub = u.astype(BF16)
        u_scr[...] = ub
        glr = jnp.dot(ub, wglr_ref[0], preferred_element_type=F32)
        z = jnp.dot(glr.astype(BF16), wg2_ref[0], preferred_element_type=F32) + bg2_ref[0]
        g_ref[0] = _log_sigmoid(z) * (1.0 / GLA_TAU)

    p_ref[0] = jnp.dot(u_scr[...], w_ref[0], preferred_element_type=F32)


def _inproj_call(x, sc, sh, w_main, w_glr, w_g2, b_g2, layer, *, tm, tn):
    bsz, seq, d = x.shape
    n = w_main.shape[2]
    dk = w_g2.shape[2]
    return pl.pallas_call(
        _inproj_kernel,
        grid=(bsz, seq // tm, n // tn),
        in_specs=[
            pl.BlockSpec((1, tm, d), lambda b, i, j: (b, i, 0)),
            pl.BlockSpec((1, 1, d), lambda b, i, j: (b, 0, 0)),
            pl.BlockSpec((1, 1, d), lambda b, i, j: (b, 0, 0)),
            pl.BlockSpec((1, d, tn), lambda b, i, j: (layer, 0, j)),
            pl.BlockSpec((1, d, LANES), lambda b, i, j: (layer, 0, 0)),
            pl.BlockSpec((1, LANES, dk), lambda b, i, j: (layer, 0, 0)),
            pl.BlockSpec((1, 1, dk), lambda b, i, j: (layer, 0, 0)),
        ],
        out_specs=[
            pl.BlockSpec((1, tm, tn), lambda b, i, j: (b, i, j)),
            pl.BlockSpec((1, tm, dk), lambda b, i, j: (b, i, 0)),
        ],
        out_shape=[
            jax.ShapeDtypeStruct((bsz, seq, n), F32),
            jax.ShapeDtypeStruct((bsz, seq, dk), F32),
        ],
        scratch_shapes=[pltpu.VMEM((tm, d), BF16)],
        compiler_params=_params(("arbitrary", "arbitrary", "arbitrary")),
        name="in_proj",
    )(x, sc, sh, w_main, w_glr, w_g2, b_g2)


def _mixer_kernel(rgx_ref, gate_ref, q_ref, k_ref, v_ref, r_ref, g_ref,
                  convw_ref, convb_ref, wg_ref, ba_ref, bx_ref, lam_ref, normg_ref,
                  out_ref, xtail_scr, h_scr, st_scr, *, ts):
    @pl.when(pl.program_id(1) == 0)
    def _():
        xtail_scr[...] = jnp.zeros_like(xtail_scr)
        h_scr[...] = jnp.zeros_like(h_scr)
        st_scr[...] = jnp.zeros_like(st_scr)

    d_rg = rgx_ref.shape[2]
    row = lax.broadcasted_iota(jnp.int32, (ts, LANES), 0)
    row_in_group = row & (SUBLANES - 1)
    row8 = lax.broadcasted_iota(jnp.int32, (SUBLANES, LANES), 0)

    for hb in range(RG_BLOCKS):
        cols = slice(hb * LANES, (hb + 1) * LANES)
        x = rgx_ref[0, :, cols]
        cw = convw_ref[0, :, cols]
        cb = convb_ref[0, :, cols]
        y = cb + cw[0:1] * pltpu.roll(x, CONV_W - 1, 0)
        for j in range(1, CONV_W - 1):
            y = y + cw[j:j + 1] * pltpu.roll(x, CONV_W - 1 - j, 0)
        y = y + cw[CONV_W - 1:CONV_W] * x
        x0 = x[0:SUBLANES]
        prev = xtail_scr[:, cols]
        y0 = None
        for j in range(CONV_W):
            d = CONV_W - 1 - j
            if d == 0:
                xs = x0
            else:
                xs = jnp.where(row8 < d, pltpu.roll(prev, d, 0), pltpu.roll(x0, d, 0))
            y0 = (cb + cw[j:j + 1] * xs) if y0 is None else (y0 + cw[j:j + 1] * xs)
        y = jnp.concatenate([y0, y[SUBLANES:]], axis=0)
        xtail_scr[:, cols] = x[ts - SUBLANES:ts]

        gz = jnp.dot(y.astype(BF16), wg_ref[0, hb], preferred_element_type=F32)
        rr = jax.nn.sigmoid(gz[:, :LANES] + ba_ref[0, :, cols])
        ii = jax.nn.sigmoid(gz[:, LANES:] + bx_ref[0, :, cols])
        log_a = (-RG_C) * rr * _softplus(-lam_ref[0, :, cols])
        a = jnp.exp(log_a)
        u = jnp.sqrt(-jnp.tanh(log_a) * (1.0 + a * a)) * ii * y

        for d in (1, 2, 4):
            m = row_in_group >= d
            a_s = jnp.where(m, pltpu.roll(a, d, 0), 1.0)
            u_s = jnp.where(m, pltpu.roll(u, d, 0), 0.0)
            u = a * u_s + u
            a = a * a_s
        hprev = h_scr[:, cols]
        pieces = []
        for grp in range(ts // SUBLANES):
            rows = slice(grp * SUBLANES, (grp + 1) * SUBLANES)
            hg = u[rows] + a[rows] * hprev
            pieces.append(hg)
            hprev = hg[SUBLANES - 1:SUBLANES]
        h_scr[:, cols] = hprev
        h = jnp.concatenate(pieces, axis=0)
        out_ref[0, :, cols] = (h * jax.nn.gelu(gate_ref[0, :, cols])).astype(out_ref.dtype)

    dk = q_ref.shape[2]
    hk = dk // GLA_HEADS
    hv = v_ref.shape[2] // GLA_HEADS
    rowc = lax.broadcasted_iota(jnp.int32, (GLA_CHUNK, dk), 0)
    causal = (lax.broadcasted_iota(jnp.int32, (GLA_CHUNK, GLA_CHUNK), 0)
              >= lax.broadcasted_iota(jnp.int32, (GLA_CHUNK, GLA_CHUNK), 1))
    for c in range(ts // GLA_CHUNK):
        rows = slice(c * GLA_CHUNK, (c + 1) * GLA_CHUNK)
        bcum = g_ref[0, rows, :]
        d = 1
        while d < GLA_CHUNK:
            bcum = bcum + jnp.where(rowc >= d, pltpu.roll(bcum, d, 0), 0.0)
            d *= 2
        b_last = bcum[GLA_CHUNK - 1:GLA_CHUNK]
        q = q_ref[0, rows, :]
        k = k_ref[0, rows, :]
        q_dec = (q * (hk ** -0.5)) * jnp.exp(bcum)
        k_inv = k * jnp.exp(-bcum)
        k_tail = k * jnp.exp(b_last - bcum)
        decay = jnp.exp(b_last)
        for hh in range(GLA_HEADS):
            kc = slice(hh * hk, (hh + 1) * hk)
            vc = slice(hh * hv, (hh + 1) * hv)
            qd = q_dec[:, kc].astype(BF16)
            ki = k_inv[:, kc].astype(BF16)
            kt = k_tail[:, kc].astype(BF16)
            vv = v_ref[0, rows, vc].astype(BF16)
            att = lax.dot_general(qd, ki, (((1,), (1,)), ((), ())),
                                  preferred_element_type=F32)
            att = jnp.where(causal, att, 0.0)
            st = st_scr[hh]
            o = jnp.dot(att.astype(BF16), vv, preferred_element_type=F32)
            o = o + lax.dot_general(qd, st.astype(BF16), (((1,), (1,)), ((), ())),
                                    preferred_element_type=F32)
            upd = lax.dot_general(vv, kt, (((0,), (0,)), ((), ())),
                                  preferred_element_type=F32)
            st_scr[hh] = st * decay[:, kc] + upd
            o = o * lax.rsqrt(jnp.mean(o * o, axis=-1, keepdims=True) + LN_EPS)
            o = o * normg_ref[0, :, vc] * jax.nn.silu(r_ref[0, rows, vc])
            out_ref[0, rows, d_rg + hh * hv:d_rg + (hh + 1) * hv] = o.astype(out_ref.dtype)


def _mixer_call(p, g, conv_w, conv_b, w_gates, b_a, b_x, lam, norm_g, layer, *, ts):
    bsz, seq, _ = p.shape
    d_rg = conv_w.shape[2]
    dk = g.shape[2]
    dv = norm_g.shape[2]
    hk = dk // GLA_HEADS
    hv = dv // GLA_HEADS
    off_q = 2 * d_rg
    off_v = off_q + 2 * dk

    def col(width, offset):
        return pl.BlockSpec((1, ts, width), lambda b, s: (b, s, offset // width))

    def par(shape):
        nd = len(shape)
        return pl.BlockSpec((1,) + shape, lambda b, s: (layer,) + (0,) * nd)

    return pl.pallas_call(
        functools.partial(_mixer_kernel, ts=ts),
        grid=(bsz, seq // ts),
        in_specs=[
            col(d_rg, 0), col(d_rg, d_rg), col(dk, off_q), col(dk, off_q + dk),
            col(dv, off_v), col(dv, off_v + dv),
            pl.BlockSpec((1, ts, dk), lambda b, s: (b, s, 0)),
            par((CONV_W, d_rg)), par((1, d_rg)), par((RG_BLOCKS, LANES, 2 * LANES)),
            par((1, d_rg)), par((1, d_rg)), par((1, d_rg)), par((1, dv)),
        ],
        out_specs=pl.BlockSpec((1, ts, d_rg + dv), lambda b, s: (b, s, 0)),
        out_shape=jax.ShapeDtypeStruct((bsz, seq, d_rg + dv), BF16),
        scratch_shapes=[
            pltpu.VMEM((SUBLANES, d_rg), F32),
            pltpu.VMEM((1, d_rg), F32),
            pltpu.VMEM((GLA_HEADS, hv, hk), F32),
        ],
        compiler_params=_params(("arbitrary", "arbitrary")),
        name="mixer",
    )(p, p, p, p, p, p, g, conv_w, conv_b, w_gates, b_a, b_x, lam, norm_g)


def _outproj_kernel(mix_ref, x_ref, gt_ref, w_ref, lg_ref, lb_ref, o_ref, *, alpha):
    y = jnp.dot(mix_ref[0], w_ref[0], preferred_element_type=F32)
    z = alpha * x_ref[0] + (1.0 + gt_ref[0]) * y
    o_ref[0] = _layer_norm_rows(z, lg_ref[0], lb_ref[0])


def _outproj_call(mix, x, gt, w_out, ln_g, ln_b, layer, *, tm, alpha):
    bsz, seq, d = x.shape
    dm = mix.shape[2]
    return pl.pallas_call(
        functools.partial(_outproj_kernel, alpha=alpha),
        grid=(bsz, seq // tm),
        in_specs=[
            pl.BlockSpec((1, tm, dm), lambda b, i: (b, i, 0)),
            pl.BlockSpec((1, tm, d), lambda b, i: (b, i, 0)),
            pl.BlockSpec((1, 1, d), lambda b, i: (b, 0, 0)),
            pl.BlockSpec((1, dm, d), lambda b, i: (layer, 0, 0)),
            pl.BlockSpec((1, 1, d), lambda b, i: (layer, 0, 0)),
            pl.BlockSpec((1, 1, d), lambda b, i: (layer, 0, 0)),
        ],
        out_specs=pl.BlockSpec((1, tm, d), lambda b, i: (b, i, 0)),
        out_shape=jax.ShapeDtypeStruct((bsz, seq, d), F32),
        compiler_params=_params(("arbitrary", "arbitrary")),
        name="out_proj_ln",
    )(mix, x, gt, w_out, ln_g, ln_b)


def _mlp_kernel(x_ref, sc_ref, sh_ref, gt_ref, wu_ref, bu_ref, wd_ref, bd_ref,
                lg_ref, lb_ref, o_ref, u_scr, acc_scr, *, alpha):
    j = pl.program_id(2)

    @pl.when(j == 0)
    def _():
        u = x_ref[0] * (1.0 + sc_ref[0]) + sh_ref[0]
        u_scr[...] = u.astype(BF16)
        acc_scr[...] = jnp.zeros_like(acc_scr)

    hdn = jnp.dot(u_scr[...], wu_ref[0], preferred_element_type=F32) + bu_ref[0]
    hdn = jnp.maximum(hdn, 0.0)
    hdn = (hdn * hdn).astype(BF16)
    acc_scr[...] += jnp.dot(hdn, wd_ref[0], preferred_element_type=F32)

    @pl.when(j == pl.num_programs(2) - 1)
    def _():
        y = acc_scr[...] + bd_ref[0]
        z = alpha * x_ref[0] + (1.0 + gt_ref[0]) * y
        o_ref[0] = _layer_norm_rows(z, lg_ref[0], lb_ref[0])


def _mlp_call(x, sc, sh, gt, w_up, b_up, w_down, b_down, ln_g, ln_b, layer, *, tm, tf, alpha):
    bsz, seq, d = x.shape
    dff = w_up.shape[2]
    return pl.pallas_call(
        functools.partial(_mlp_kernel, alpha=alpha),
        grid=(bsz, seq // tm, dff // tf),
        in_specs=[
            pl.BlockSpec((1, tm, d), lambda b, i, j: (b, i, 0)),
            pl.BlockSpec((1, 1, d), lambda b, i, j: (b, 0, 0)),
            pl.BlockSpec((1, 1, d), lambda b, i, j: (b, 0, 0)),
            pl.BlockSpec((1, 1, d), lambda b, i, j: (b, 0, 0)),
            pl.BlockSpec((1, d, tf), lambda b, i, j: (layer, 0, j)),
            pl.BlockSpec((1, 1, tf), lambda b, i, j: (layer, 0, j)),
            pl.BlockSpec((1, tf, d), lambda b, i, j: (layer, j, 0)),
            pl.BlockSpec((1, 1, d), lambda b, i, j: (layer, 0, 0)),
            pl.BlockSpec((1, 1, d), lambda b, i, j: (layer, 0, 0)),
            pl.BlockSpec((1, 1, d), lambda b, i, j: (layer, 0, 0)),
        ],
        out_specs=pl.BlockSpec((1, tm, d), lambda b, i, j: (b, i, 0)),
        out_shape=jax.ShapeDtypeStruct((bsz, seq, d), F32),
        scratch_shapes=[pltpu.VMEM((tm, d), BF16), pltpu.VMEM((tm, d), F32)],
        compiler_params=_params(("arbitrary", "arbitrary", "arbitrary")),
        name="mlp_ln",
    )(x, sc, sh, gt, w_up, b_up, w_down, b_down, ln_g, ln_b)


def kernel(x, c, w_ada, b_ada, w_in, conv_w, conv_b, w_rg_a, b_rg_a, w_rg_x, b_rg_x, rg_lambda,
           w_gla_g2, b_gla_g2, gla_norm_g, w_out, ln1_g, ln1_b, w_up, b_up, w_down, b_down,
           ln2_g, ln2_b):
    depth, d, _ = w_ada.shape
    bsz = x.shape[0]
    d_rg = conv_w.shape[2]
    dk = w_gla_g2.shape[2]
    dv = gla_norm_g.shape[1]
    n_main = 2 * d_rg + 2 * dk + 2 * dv
    assert w_in.shape[2] == n_main + GLA_RANK
    alpha = (2 * depth) ** 0.25

    c_pad = jnp.zeros((SUBLANES, d), F32).at[:bsz].set(c)
    mod = _ada_call(c_pad, w_ada, b_ada.reshape(depth, 1, N_MOD * d), tn=1024)
    mod = mod[:, :bsz].reshape(depth, bsz, N_MOD, 1, d)

    w_main = w_in[:, :, :n_main].astype(BF16)
    w_glr = jnp.pad(w_in[:, :, n_main:], ((0, 0), (0, 0), (0, LANES - GLA_RANK))).astype(BF16)
    w_g2 = jnp.pad(w_gla_g2, ((0, 0), (0, LANES - GLA_RANK), (0, 0))).astype(BF16)
    w_gates = jnp.concatenate([w_rg_a, w_rg_x], axis=-1).astype(BF16)
    w_out_b = w_out.astype(BF16)
    w_up_b = w_up.astype(BF16)
    w_down_b = w_down.astype(BF16)

    def row3(a):
        return a.reshape(depth, 1, a.shape[-1])

    b_g2, conv_b3, b_a3, b_x3, lam3, normg3 = map(
        row3, (b_gla_g2, conv_b, b_rg_a, b_rg_x, rg_lambda, gla_norm_g))
    ln1_g3, ln1_b3, ln2_g3, ln2_b3, b_up3, b_down3 = map(
        row3, (ln1_g, ln1_b, ln2_g, ln2_b, b_up, b_down))

    for l in range(depth):
        sh1, sc1, gt1, sh2, sc2, gt2 = [mod[l, :, m] for m in range(N_MOD)]
        p, g = _inproj_call(x, sc1, sh1, w_main, w_glr, w_g2, b_g2, l, tm=1024, tn=1280)
        mix = _mixer_call(p, g, conv_w, conv_b3, w_gates, b_a3, b_x3, lam3, normg3, l, ts=256)
        x = _outproj_call(mix, x, gt1, w_out_b, ln1_g3, ln1_b3, l, tm=512, alpha=alpha)
        x = _mlp_call(x, sc2, sh2, gt2, w_up_b, b_up3, w_down_b, b_down3, ln2_g3, ln2_b3, l,
                      tm=512, tf=512, alpha=alpha)
    return x
```

```python
import functools

import jax
import jax.numpy as jnp
from jax import lax
from jax.experimental import pallas as pl
from jax.experimental.pallas import tpu as pltpu

F32 = jnp.float32
BF16 = jnp.bfloat16

LANES = 128
SUBLANES = 8
VMEM_LIMIT_BYTES = 56 * 1024 * 1024

RG_BLOCKS = 8
CONV_W = 4
RG_C = 8.0
GLA_HEADS = 4
GLA_RANK = 16
GLA_TAU = 16.0
GLA_CHUNK = 64
N_MOD = 6
LN_EPS = 1e-5


def _params(semantics):
    return pltpu.CompilerParams(dimension_semantics=semantics,
                                vmem_limit_bytes=VMEM_LIMIT_BYTES)


def _layer_norm_rows(z, g, b):
    mu = jnp.mean(z, axis=-1, keepdims=True)
    zc = z - mu
    var = jnp.mean(zc * zc, axis=-1, keepdims=True)
    return zc * lax.rsqrt(var + LN_EPS) * g + b


def _log_sigmoid(z):
    return jnp.minimum(z, 0.0) - jnp.log1p(jnp.exp(-jnp.abs(z)))


def _softplus(z):
    return jnp.maximum(z, 0.0) + jnp.log1p(jnp.exp(-jnp.abs(z)))


def _ada_kernel(c_ref, w_ref, b_ref, o_ref):
    c = c_ref[...]
    ca = (c * jax.nn.sigmoid(c)).astype(BF16)
    o_ref[0] = jnp.dot(ca, w_ref[0].astype(BF16), preferred_element_type=F32) + b_ref[0]


def _ada_call(c_pad, w_ada, b_ada3, *, tn):
    depth, d, n = w_ada.shape
    rows = c_pad.shape[0]
    return pl.pallas_call(
        _ada_kernel,
        grid=(depth, n // tn),
        in_specs=[
            pl.BlockSpec((rows, d), lambda l, j: (0, 0)),
            pl.BlockSpec((1, d, tn), lambda l, j: (l, 0, j)),
            pl.BlockSpec((1, 1, tn), lambda l, j: (l, 0, j)),
        ],
        out_specs=pl.BlockSpec((1, rows, tn), lambda l, j: (l, 0, j)),
        out_shape=jax.ShapeDtypeStruct((depth, rows, n), F32),
        compiler_params=_params(("arbitrary", "arbitrary")),
        name="ada_mod",
    )(c_pad, w_ada, b_ada3)


def _inproj_kernel(x_ref, sc_ref, sh_ref, w_ref, wglr_ref, wg2_ref, bg2_ref, p_ref, g_ref):
    ub = (x_ref[0] * (1.0 + sc_ref[0]) + sh_ref[0]).astype(BF16)
    p_ref[0] = jnp.dot(ub, w_ref[0], preferred_element_type=F32)
    glr = jnp.dot(ub, wglr_ref[0], preferred_element_type=F32)
    z = jnp.dot(glr.astype(BF16), wg2_ref[0], preferred_element_type=F32) + bg2_ref[0]
    g_ref[0] = _log_sigmoid(z) * (1.0 / GLA_TAU)


def _resident(block_shape, index_map):
    return pl.BlockSpec(block_shape, index_map, pipeline_mode=pl.Buffered(1))


def _inproj_call(x, sc, sh, w_in, n_main, w_glr, w_g2, b_g2, layer, *, tm):
    bsz, seq, d = x.shape
    dk = w_g2.shape[2]
    return pl.pallas_call(
        _inproj_kernel,
        grid=(bsz, seq // tm),
        in_specs=[
            pl.BlockSpec((1, tm, d), lambda b, i: (b, i, 0)),
            pl.BlockSpec((1, 1, d), lambda b, i: (b, 0, 0)),
            pl.BlockSpec((1, 1, d), lambda b, i: (b, 0, 0)),
            _resident((1, d, n_main), lambda b, i: (layer, 0, 0)),
            _resident((1, d, LANES), lambda b, i: (layer, 0, 0)),
            _resident((1, LANES, dk), lambda b, i: (layer, 0, 0)),
            _resident((1, 1, dk), lambda b, i: (layer, 0, 0)),
        ],
        out_specs=[
            pl.BlockSpec((1, tm, n_main), lambda b, i: (b, i, 0)),
            pl.BlockSpec((1, tm, dk), lambda b, i: (b, i, 0)),
        ],
        out_shape=[
            jax.ShapeDtypeStruct((bsz, seq, n_main), F32),
            jax.ShapeDtypeStruct((bsz, seq, dk), F32),
        ],
        compiler_params=_params(("arbitrary", "arbitrary")),
        name="in_proj",
    )(x, sc, sh, w_in, w_glr, w_g2, b_g2)


def _mixer_kernel(rgx_ref, gate_ref, q_ref, k_ref, v_ref, r_ref, g_ref,
                  convw_ref, convb_ref, wg_ref, ba_ref, bx_ref, lam_ref, normg_ref,
                  out_ref, xtail_scr, h_scr, st_scr, *, ts):
    @pl.when(pl.program_id(1) == 0)
    def _():
        xtail_scr[...] = jnp.zeros_like(xtail_scr)
        h_scr[...] = jnp.zeros_like(h_scr)
        st_scr[...] = jnp.zeros_like(st_scr)

    d_rg = rgx_ref.shape[2]
    row = lax.broadcasted_iota(jnp.int32, (ts, LANES), 0)
    row_in_group = row & (SUBLANES - 1)
    row8 = lax.broadcasted_iota(jnp.int32, (SUBLANES, LANES), 0)

    for hb in range(RG_BLOCKS):
        cols = slice(hb * LANES, (hb + 1) * LANES)
        x = rgx_ref[0, :, cols]
        cw = convw_ref[0, :, cols]
        cb = convb_ref[0, :, cols]
        y = cb + cw[0:1] * pltpu.roll(x, CONV_W - 1, 0)
        for j in range(1, CONV_W - 1):
            y = y + cw[j:j + 1] * pltpu.roll(x, CONV_W - 1 - j, 0)
        y = y + cw[CONV_W - 1:CONV_W] * x
        x0 = x[0:SUBLANES]
        prev = xtail_scr[:, cols]
        y0 = None
        for j in range(CONV_W):
            d = CONV_W - 1 - j
            if d == 0:
                xs = x0
            else:
                xs = jnp.where(row8 < d, pltpu.roll(prev, d, 0), pltpu.roll(x0, d, 0))
            y0 = (cb + cw[j:j + 1] * xs) if y0 is None else (y0 + cw[j:j + 1] * xs)
        y = jnp.concatenate([y0, y[SUBLANES:]], axis=0)
        xtail_scr[:, cols] = x[ts - SUBLANES:ts]

        gz = jnp.dot(y.astype(BF16), wg_ref[0, hb], preferred_element_type=F32)
        rr = jax.nn.sigmoid(gz[:, :LANES] + ba_ref[0, :, cols])
        ii = jax.nn.sigmoid(gz[:, LANES:] + bx_ref[0, :, cols])
        log_a = (-RG_C) * rr * _softplus(-lam_ref[0, :, cols])
        a = jnp.exp(log_a)
        u = jnp.sqrt(-jnp.tanh(log_a) * (1.0 + a * a)) * ii * y

        for d in (1, 2, 4):
            m = row_in_group >= d
            a_s = jnp.where(m, pltpu.roll(a, d, 0), 1.0)
            u_s = jnp.where(m, pltpu.roll(u, d, 0), 0.0)
            u = a * u_s + u
            a = a * a_s
        hprev = h_scr[:, cols]
        pieces = []
        for grp in range(ts // SUBLANES):
            rows = slice(grp * SUBLANES, (grp + 1) * SUBLANES)
            hg = u[rows] + a[rows] * hprev
            pieces.append(hg)
            hprev = hg[SUBLANES - 1:SUBLANES]
        h_scr[:, cols] = hprev
        h = jnp.concatenate(pieces, axis=0)
        out_ref[0, :, cols] = (h * jax.nn.gelu(gate_ref[0, :, cols])).astype(out_ref.dtype)

    dk = q_ref.shape[2]
    hk = dk // GLA_HEADS
    hv = v_ref.shape[2] // GLA_HEADS
    rowc = lax.broadcasted_iota(jnp.int32, (GLA_CHUNK, dk), 0)
    causal = (lax.broadcasted_iota(jnp.int32, (GLA_CHUNK, GLA_CHUNK), 0)
              >= lax.broadcasted_iota(jnp.int32, (GLA_CHUNK, GLA_CHUNK), 1))
    for c in range(ts // GLA_CHUNK):
        rows = slice(c * GLA_CHUNK, (c + 1) * GLA_CHUNK)
        bcum = g_ref[0, rows, :]
        d = 1
        while d < GLA_CHUNK:
            bcum = bcum + jnp.where(rowc >= d, pltpu.roll(bcum, d, 0), 0.0)
            d *= 2
        b_last = bcum[GLA_CHUNK - 1:GLA_CHUNK]
        q = q_ref[0, rows, :]
        k = k_ref[0, rows, :]
        q_dec = (q * (hk ** -0.5)) * jnp.exp(bcum)
        k_inv = k * jnp.exp(-bcum)
        k_tail = k * jnp.exp(b_last - bcum)
        decay = jnp.exp(b_last)
        for hh in range(GLA_HEADS):
            kc = slice(hh * hk, (hh + 1) * hk)
            vc = slice(hh * hv, (hh + 1) * hv)
            qd = q_dec[:, kc].astype(BF16)
            ki = k_inv[:, kc].astype(BF16)
            kt = k_tail[:, kc].astype(BF16)
            vv = v_ref[0, rows, vc].astype(BF16)
            att = lax.dot_general(qd, ki, (((1,), (1,)), ((), ())),
                                  preferred_element_type=F32)
            att = jnp.where(causal, att, 0.0)
            st = st_scr[hh]
            o = jnp.dot(att.astype(BF16), vv, preferred_element_type=F32)
            o = o + lax.dot_general(qd, st.astype(BF16), (((1,), (1,)), ((), ())),
                                    preferred_element_type=F32)
            upd = lax.dot_general(vv, kt, (((0,), (0,)), ((), ())),
                                  preferred_element_type=F32)
            st_scr[hh] = st * decay[:, kc] + upd
            o = o * lax.rsqrt(jnp.mean(o * o, axis=-1, keepdims=True) + LN_EPS)
            o = o * normg_ref[0, :, vc] * jax.nn.silu(r_ref[0, rows, vc])
            out_ref[0, rows, d_rg + hh * hv:d_rg + (hh + 1) * hv] = o.astype(out_ref.dtype)


def _mixer_call(p, g, conv_w, conv_b, w_gates, b_a, b_x, lam, norm_g, layer, *, ts):
    bsz, seq, _ = p.shape
    d_rg = conv_w.shape[2]
    dk = g.shape[2]
    dv = norm_g.shape[2]
    hk = dk // GLA_HEADS
    hv = dv // GLA_HEADS
    off_q = 2 * d_rg
    off_v = off_q + 2 * dk

    def col(width, offset):
        return pl.BlockSpec((1, ts, width), lambda b, s: (b, s, offset // width))

    def par(shape):
        nd = len(shape)
        return pl.BlockSpec((1,) + shape, lambda b, s: (layer,) + (0,) * nd)

    return pl.pallas_call(
        functools.partial(_mixer_kernel, ts=ts),
        grid=(bsz, seq // ts),
        in_specs=[
            col(d_rg, 0), col(d_rg, d_rg), col(dk, off_q), col(dk, off_q + dk),
            col(dv, off_v), col(dv, off_v + dv),
            pl.BlockSpec((1, ts, dk), lambda b, s: (b, s, 0)),
            par((CONV_W, d_rg)), par((1, d_rg)), par((RG_BLOCKS, LANES, 2 * LANES)),
            par((1, d_rg)), par((1, d_rg)), par((1, d_rg)), par((1, dv)),
        ],
        out_specs=pl.BlockSpec((1, ts, d_rg + dv), lambda b, s: (b, s, 0)),
        out_shape=jax.ShapeDtypeStruct((bsz, seq, d_rg + dv), BF16),
        scratch_shapes=[
            pltpu.VMEM((SUBLANES, d_rg), F32),
            pltpu.VMEM((1, d_rg), F32),
            pltpu.VMEM((GLA_HEADS, hv, hk), F32),
        ],
        compiler_params=_params(("arbitrary", "arbitrary")),
        name="mixer",
    )(p, p, p, p, p, p, g, conv_w, conv_b, w_gates, b_a, b_x, lam, norm_g)


def _outproj_kernel(mix_ref, x_ref, gt_ref, w_ref, lg_ref, lb_ref, o_ref, *, alpha, sub):
    gate = 1.0 + gt_ref[0]
    for k in range(mix_ref.shape[1] // sub):
        rows = slice(k * sub, (k + 1) * sub)
        y = jnp.dot(mix_ref[0, rows, :], w_ref[0], preferred_element_type=F32)
        z = alpha * x_ref[0, rows, :] + gate * y
        o_ref[0, rows, :] = _layer_norm_rows(z, lg_ref[0], lb_ref[0])


def _outproj_call(mix, x, gt, w_out, ln_g, ln_b, layer, *, tm, sub, alpha):
    bsz, seq, d = x.shape
    dm = mix.shape[2]
    return pl.pallas_call(
        functools.partial(_outproj_kernel, alpha=alpha, sub=sub),
        grid=(bsz, seq // tm),
        in_specs=[
            pl.BlockSpec((1, tm, dm), lambda b, i: (b, i, 0)),
            pl.BlockSpec((1, tm, d), lambda b, i: (b, i, 0)),
            pl.BlockSpec((1, 1, d), lambda b, i: (b, 0, 0)),
            _resident((1, dm, d), lambda b, i: (layer, 0, 0)),
            _resident((1, 1, d), lambda b, i: (layer, 0, 0)),
            _resident((1, 1, d), lambda b, i: (layer, 0, 0)),
        ],
        out_specs=pl.BlockSpec((1, tm, d), lambda b, i: (b, i, 0)),
        out_shape=jax.ShapeDtypeStruct((bsz, seq, d), F32),
        compiler_params=_params(("arbitrary", "arbitrary")),
        name="out_proj_ln",
    )(mix, x, gt, w_out, ln_g, ln_b)


def _mlp_kernel(x_ref, sc_ref, sh_ref, gt_ref, wu_ref, bu_ref, wd_ref, bd_ref,
                lg_ref, lb_ref, o_ref, u_scr, *, alpha):
    j = pl.program_id(2)

    @pl.when(j == 0)
    def _():
        u = x_ref[0] * (1.0 + sc_ref[0]) + sh_ref[0]
        u_scr[...] = u.astype(BF16)
        o_ref[...] = jnp.zeros_like(o_ref)

    hdn = jnp.dot(u_scr[...], wu_ref[0], preferred_element_type=F32) + bu_ref[0]
    hdn = jnp.maximum(hdn, 0.0)
    hdn = (hdn * hdn).astype(BF16)
    o_ref[0] += jnp.dot(hdn, wd_ref[0], preferred_element_type=F32)

    @pl.when(j == pl.num_programs(2) - 1)
    def _():
        y = o_ref[0] + bd_ref[0]
        z = alpha * x_ref[0] + (1.0 + gt_ref[0]) * y
        o_ref[0] = _layer_norm_rows(z, lg_ref[0], lb_ref[0])


def _mlp_call(x, sc, sh, gt, w_up, b_up, w_down, b_down, ln_g, ln_b, layer, *, tm, tf, alpha):
    bsz, seq, d = x.shape
    dff = w_up.shape[2]
    return pl.pallas_call(
        functools.partial(_mlp_kernel, alpha=alpha),
        grid=(bsz, seq // tm, dff // tf),
        in_specs=[
            pl.BlockSpec((1, tm, d), lambda b, i, j: (b, i, 0)),
            pl.BlockSpec((1, 1, d), lambda b, i, j: (b, 0, 0)),
            pl.BlockSpec((1, 1, d), lambda b, i, j: (b, 0, 0)),
            pl.BlockSpec((1, 1, d), lambda b, i, j: (b, 0, 0)),
            pl.BlockSpec((1, d, tf), lambda b, i, j: (layer, 0, j)),
            pl.BlockSpec((1, 1, tf), lambda b, i, j: (layer, 0, j)),
            pl.BlockSpec((1, tf, d), lambda b, i, j: (layer, j, 0)),
            pl.BlockSpec((1, 1, d), lambda b, i, j: (layer, 0, 0)),
            pl.BlockSpec((1, 1, d), lambda b, i, j: (layer, 0, 0)),
            pl.BlockSpec((1, 1, d), lambda b, i, j: (layer, 0, 0)),
        ],
        out_specs=pl.BlockSpec((1, tm, d), lambda b, i, j: (b, i, 0)),
        out_shape=jax.ShapeDtypeStruct((bsz, seq, d), F32),
        scratch_shapes=[pltpu.VMEM((tm, d), BF16)],
        compiler_params=_params(("arbitrary", "arbitrary", "arbitrary")),
        name="mlp_ln",
    )(x, sc, sh, gt, w_up, b_up, w_down, b_down, ln_g, ln_b)


def kernel(x, c, w_ada, b_ada, w_in, conv_w, conv_b, w_rg_a, b_rg_a, w_rg_x, b_rg_x, rg_lambda,
           w_gla_g2, b_gla_g2, gla_norm_g, w_out, ln1_g, ln1_b, w_up, b_up, w_down, b_down,
           ln2_g, ln2_b):
    depth, d, _ = w_ada.shape
    bsz = x.shape[0]
    d_rg = conv_w.shape[2]
    dk = w_gla_g2.shape[2]
    dv = gla_norm_g.shape[1]
    n_main = 2 * d_rg + 2 * dk + 2 * dv
    assert w_in.shape[2] == n_main + GLA_RANK
    alpha = (2 * depth) ** 0.25

    c_pad = jnp.zeros((SUBLANES, d), F32).at[:bsz].set(c)
    mod = _ada_call(c_pad, w_ada, b_ada.reshape(depth, 1, N_MOD * d), tn=1024)
    mod = mod[:, :bsz].reshape(depth, bsz, N_MOD, 1, d)

    w_in_b = w_in.astype(BF16)
    w_glr = jnp.pad(w_in[:, :, n_main:], ((0, 0), (0, 0), (0, LANES - GLA_RANK))).astype(BF16)
    w_g2 = jnp.pad(w_gla_g2, ((0, 0), (0, LANES - GLA_RANK), (0, 0))).astype(BF16)
    w_gates = jnp.concatenate([w_rg_a, w_rg_x], axis=-1).astype(BF16)
    w_out_b = w_out.astype(BF16)
    w_up_b = w_up.astype(BF16)
    w_down_b = w_down.astype(BF16)

    def row3(a):
        return a.reshape(depth, 1, a.shape[-1])

    b_g2, conv_b3, b_a3, b_x3, lam3, normg3 = map(
        row3, (b_gla_g2, conv_b, b_rg_a, b_rg_x, rg_lambda, gla_norm_g))
    ln1_g3, ln1_b3, ln2_g3, ln2_b3, b_up3, b_down3 = map(
        row3, (ln1_g, ln1_b, ln2_g, ln2_b, b_up, b_down))

    for l in range(depth):
        sh1, sc1, gt1, sh2, sc2, gt2 = [mod[l, :, m] for m in range(N_MOD)]
        p, g = _inproj_call(x, sc1, sh1, w_in_b, n_main, w_glr, w_g2, b_g2, l, tm=256)
        mix = _mixer_call(p, g, conv_w, conv_b3, w_gates, b_a3, b_x3, lam3, normg3, l, ts=256)
        x = _outproj_call(mix, x, gt1, w_out_b, ln1_g3, ln1_b3, l, tm=512, sub=128, alpha=alpha)
        x = _mlp_call(x, sc2, sh2, gt2, w_up_b, b_up3, w_down_b, b_down3, ln2_g3, ln2_b3, l,
                      tm=1024, tf=512, alpha=alpha)
    return x
```
